```python
import math
import jax, jax.numpy as jnp
from jax import lax
import numpy as np

D_MODEL = 4096
BATCH = 32
SEQ = 256
DEPTH = 4
DEC_BATCH = 8
DEC_SEQ = 1024
PAST_LEN = 256

GRID_W = 64
N_BRANCH = 4
MIX_W = D_MODEL // N_BRANCH
HEAD_DIM = 128
H_A = MIX_W // HEAD_DIM
KV_A = max(1, H_A // 4)
H_B = MIX_W // HEAD_DIM
NA_ROWS = 8
NA_COLS = 16
H_C = MIX_W // HEAD_DIM
DK = HEAD_DIM
DV = HEAD_DIM
SHORT_CONV = 3
CHUNK = 64
W_D = MIX_W
CONV_D = 31
D_FF = ((8 * D_MODEL // 3 + 255) // 256) * 256
Q_BLOCK = 128
ROPE_THETA = 10000.0
EPS = 1e-6
NEG_INF = -1e30
SPLIT_SIZES = (H_A * HEAD_DIM, KV_A * HEAD_DIM, KV_A * HEAD_DIM,
               H_B * HEAD_DIM, H_B * HEAD_DIM, H_B * HEAD_DIM,
               H_C * DK, H_C * DK, H_C * DV, H_C * DV, 2 * H_C, 2 * H_C,
               2 * W_D, N_BRANCH * D_MODEL)

kernel_name = 'hybrid_diffusion_parallel_mixer_step'


def rms_norm(x, w):
    xf = x.astype(jnp.float32)
    y = xf * lax.rsqrt(jnp.mean(xf * xf, axis=-1, keepdims=True) + EPS)
    return (y * w.astype(jnp.float32)).astype(x.dtype)


def layer_norm(x, w, b):
    xf = x.astype(jnp.float32)
    mu = jnp.mean(xf, axis=-1, keepdims=True)
    xc = xf - mu
    var = jnp.mean(xc * xc, axis=-1, keepdims=True)
    return (xc * lax.rsqrt(var + EPS) * w.astype(jnp.float32) + b.astype(jnp.float32)).astype(x.dtype)


def l2norm(x):
    xf = x.astype(jnp.float32)
    return xf * lax.rsqrt(jnp.sum(xf * xf, axis=-1, keepdims=True) + EPS)


def swiglu(x, w1, w3, w2):
    return (jax.nn.silu(x @ w1) * (x @ w3)) @ w2


def depthwise_conv(x, w):
    k, ch = w.shape
    return lax.conv_general_dilated(x, w[:, None, :], window_strides=(1,), padding=[(k // 2, k // 2)],
                                    dimension_numbers=('NWC', 'WIO', 'NWC'), feature_group_count=ch)


def axial_rope(x):
    t = x.shape[1]
    half = HEAD_DIM // 2
    nf = half // 2
    pos = jnp.arange(t)
    inv = 1.0 / (ROPE_THETA ** (jnp.arange(nf, dtype=jnp.float32) / nf))

    def rotate(xh, p):
        ang = p.astype(jnp.float32)[:, None] * inv
        cos = jnp.cos(ang)[None, :, None, :]
        sin = jnp.sin(ang)[None, :, None, :]
        x1, x2 = xh[..., :nf], xh[..., nf:]
        return jnp.concatenate([x1 * cos - x2 * sin, x1 * sin + x2 * cos], axis=-1)

    xf = x.astype(jnp.float32)
    out = jnp.concatenate([rotate(xf[..., :half], pos // GRID_W), rotate(xf[..., half:], pos % GRID_W)], axis=-1)
    return out.astype(x.dtype)


def block_attention(q, k, v):
    b, tq, hq, d = q.shape
    hkv = k.shape[2]
    grp = hq // hkv
    nb = tq // Q_BLOCK
    qb = jnp.moveaxis(q.reshape(b, nb, Q_BLOCK, hkv, grp, d), 1, 0)
    scale = d ** -0.5

    def one_block(qi):
        s = jnp.einsum('bqhgd,bkhd->bhgqk', qi, k, preferred_element_type=jnp.float32) * scale
        p = jax.nn.softmax(s, axis=-1).astype(v.dtype)
        return jnp.einsum('bhgqk,bkhd->bqhgd', p, v)

    o = lax.map(one_block, qb)
    return jnp.moveaxis(o, 0, 1).reshape(b, tq, hq, d)


def neighbourhood_attention(q, k, v, k_ctx, v_ctx, rpb):
    b, t, h, d = q.shape
    rows = t // GRID_W
    wr = min(NA_ROWS, rows)
    r = np.arange(rows)
    rs = np.clip(r - wr // 2, 0, rows - wr)
    row_idx = rs[:, None] + np.arange(wr)[None, :]
    cpos = np.arange(GRID_W)
    cs = np.clip(cpos - NA_COLS // 2, 0, GRID_W - NA_COLS)
    col_ok = (cpos[None, :] >= cs[:, None]) & (cpos[None, :] < cs[:, None] + NA_COLS)
    dr_idx = row_idx - r[:, None] + NA_ROWS - 1
    dc_idx = np.clip(cpos[None, :] - cpos[:, None] + NA_COLS - 1, 0, 2 * NA_COLS - 2)
    bias = rpb[:, dr_idx[:, None, :, None], dc_idx[None, :, None, :]]

    qg = q.reshape(b, rows, GRID_W, h, d)
    k_band = k.reshape(b, rows, GRID_W, h, d)[:, row_idx]
    v_band = v.reshape(b, rows, GRID_W, h, d)[:, row_idx]
    scale = d ** -0.5
    s_loc = jnp.einsum('brqhd,brwkhd->bhrqwk', qg, k_band, preferred_element_type=jnp.float32) * scale
    s_loc = s_loc + bias[None].astype(jnp.float32)
    s_loc = jnp.where(col_ok[None, None, None, :, None, :], s_loc, NEG_INF)
    s_ctx = jnp.einsum('brqhd,bchd->bhrqc', qg, k_ctx, preferred_element_type=jnp.float32) * scale
    n_loc = wr * GRID_W
    s = jnp.concatenate([s_loc.reshape(b, h, rows, GRID_W, n_loc), s_ctx], axis=-1)
    p = jax.nn.softmax(s, axis=-1).astype(v.dtype)
    p_loc = p[..., :n_loc].reshape(b, h, rows, GRID_W, wr, GRID_W)
    p_ctx = p[..., n_loc:]
    o = (jnp.einsum('bhrqwk,brwkhd->brqhd', p_loc, v_band)
         + jnp.einsum('bhrqc,bchd->brqhd', p_ctx, v_ctx))
    return o.reshape(b, t, h, d)


def chunk_gated_delta(q, k, v, g, beta, s0):
    b, t, h, _ = q.shape
    dv = v.shape[-1]
    n = t // CHUNK

    def to_chunks(x):
        x = x.reshape((b, n, CHUNK, h) + x.shape[3:])
        return jnp.moveaxis(x, (1, 3), (0, 2))

    qc, kc, vc, gc, bc = to_chunks(q), to_chunks(k), to_chunks(v), to_chunks(g), to_chunks(beta)
    gcum = jnp.cumsum(gc, axis=-1)
    lower = jnp.tril(jnp.ones((CHUNK, CHUNK), dtype=bool))
    strict = jnp.tril(jnp.ones((CHUNK, CHUNK), dtype=bool), -1)
    diff = gcum[..., :, None] - gcum[..., None, :]
    decay = jnp.where(lower, jnp.exp(jnp.where(lower, diff, 0.0)), 0.0)
    kb = kc * bc[..., None]
    a_mat = jnp.where(strict, jnp.einsum('nbhid,nbhjd->nbhij', kb, kc) * decay, 0.0) + jnp.eye(CHUNK, dtype=jnp.float32)
    rhs = jnp.concatenate([vc * bc[..., None], kb * jnp.exp(gcum)[..., None]], axis=-1)
    sol = lax.linalg.triangular_solve(a_mat, rhs, left_side=True, lower=True, unit_diagonal=True)
    u, w = sol[..., :dv], sol[..., dv:]
    intra = jnp.einsum('nbhid,nbhjd->nbhij', qc, kc) * decay

    def step(s, xs):
        q_i, k_i, u_i, w_i, g_i, a_i = xs
        v_new = u_i - jnp.einsum('bhck,bhkv->bhcv', w_i, s)
        o = (jnp.einsum('bhck,bhkv->bhcv', q_i * jnp.exp(g_i)[..., None], s)
             + jnp.einsum('bhij,bhjv->bhiv', a_i, v_new))
        g_last = g_i[..., -1]
        s = (s * jnp.exp(g_last)[..., None, None]
             + jnp.einsum('bhck,bhcv->bhkv', k_i * jnp.exp(g_last[..., None] - g_i)[..., None], v_new))
        return s, o

    s_fin, o = lax.scan(step, s0, (qc, kc, u, w, gcum, intra))
    o = jnp.moveaxis(o, (0, 2), (1, 3)).reshape(b, t, h, dv)
    return o, s_fin


def gated_deltanet(q, k, v, a, bgate, a_log, dt_bias, s0):
    q = l2norm(q) * (DK ** -0.5)
    k = l2norm(k)
    v = v.astype(jnp.float32)
    g = -jnp.exp(a_log.astype(jnp.float32)) * jax.nn.softplus(a.astype(jnp.float32) + dt_bias.astype(jnp.float32))
    beta = jax.nn.sigmoid(bgate.astype(jnp.float32))
    s0 = s0.astype(jnp.float32)
    o_f, s_f = chunk_gated_delta(q, k, v, g[:, :, 0], beta[:, :, 0], s0[:, 0])
    rev = lambda x: jnp.flip(x, axis=1)
    o_b, s_b = chunk_gated_delta(rev(q), rev(k), rev(v), rev(g[:, :, 1]), rev(beta[:, :, 1]), s0[:, 1])
    return o_f + rev(o_b), jnp.stack([s_f, s_b], axis=1)


def conformer_conv(u, w_dw, b_dw, ln_w, ln_b):
    a, gt = jnp.split(u, 2, axis=-1)
    y = a * jax.nn.sigmoid(gt)
    y = depthwise_conv(y, w_dw) + b_dw
    y = layer_norm(y, ln_w, ln_b)
    return jax.nn.silu(y)


def token_mixer(h, l, W, cache):
    b, t, _ = h.shape
    points = [int(p) for p in np.cumsum(SPLIT_SIZES)[:-1]]
    aq, ak, av, bq, bk, bv, cq, ck, cv, cz, ca, cb, dglu, gates = jnp.split(h @ W['w_in'][l], points, axis=-1)
    aq = rms_norm(aq.reshape(b, t, H_A, HEAD_DIM), W['qn_a'][l])
    ak = rms_norm(ak.reshape(b, t, KV_A, HEAD_DIM), W['kn_a'][l])
    av = av.reshape(b, t, KV_A, HEAD_DIM)
    bq = bq.reshape(b, t, H_B, HEAD_DIM)
    bk = bk.reshape(b, t, H_B, HEAD_DIM)
    bv = bv.reshape(b, t, H_B, HEAD_DIM)
    qkv = jax.nn.silu(depthwise_conv(jnp.concatenate([cq, ck, cv], axis=-1), W['conv_c'][l]))
    cq, ck, cv = jnp.split(qkv, 3, axis=-1)
    cq = cq.reshape(b, t, H_C, DK)
    ck = ck.reshape(b, t, H_C, DK)
    cv = cv.reshape(b, t, H_C, DV)
    ca = ca.reshape(b, t, 2, H_C)
    cb = cb.reshape(b, t, 2, H_C)
    if cache is None:
        s0 = jnp.zeros((b, 2, H_C, DK, DV), jnp.float32)
        ya = block_attention(aq, ak, av)
        yb = block_attention(bq, bk, bv)
    else:
        ctx_ak, ctx_av, ctx_bk, ctx_bv, s0 = cache
        ya = block_attention(axial_rope(aq), jnp.concatenate([axial_rope(ak), ctx_ak], axis=1),
                             jnp.concatenate([av, ctx_av], axis=1))
        yb = neighbourhood_attention(bq, bk, bv, ctx_bk, ctx_bv, W['rpb_b'][l])
    yc, s_fin = gated_deltanet(cq, ck, cv, ca, cb, W['a_log_c'][l], W['dt_bias_c'][l], s0)
    yc = rms_norm(yc.astype(h.dtype), W['onorm_c'][l]) * jax.nn.silu(cz.reshape(b, t, H_C, DV))
    yd = conformer_conv(dglu, W['conv_d'][l], W['conv_d_b'][l], W['ln_d_w'][l], W['ln_d_b'][l])
    ys = jnp.stack([ya.reshape(b, t, MIX_W), yb.reshape(b, t, MIX_W), yc.reshape(b, t, MIX_W), yd], axis=2)
    up = jnp.einsum('btnw,nwd->btnd', ys, W['w_up'][l])
    g = jax.nn.sigmoid(gates.reshape(b, t, N_BRANCH, D_MODEL))
    out = jnp.sum(g * up, axis=2) @ W['w_out'][l]
    state = None if cache is not None else (ak, av, bk, bv, s_fin.astype(h.dtype))
    return out, state


def trunk_layer(x, cond, l, W, cache):
    mod = (jax.nn.silu(cond) @ W['w_mod'][l] + W['b_mod'][l]).reshape(cond.shape[0], 9, 1, D_MODEL)
    nw = W['norm_w'][l]
    h = rms_norm(x, nw[0]) * (1 + mod[:, 1]) + mod[:, 0]
    x = x + 0.5 * mod[:, 2] * swiglu(h, W['ffn_w1'][l, 0], W['ffn_w3'][l, 0], W['ffn_w2'][l, 0])
    h = rms_norm(x, nw[1]) * (1 + mod[:, 4]) + mod[:, 3]
    y, state = token_mixer(h, l, W, cache)
    x = x + mod[:, 5] * y
    h = rms_norm(x, nw[2]) * (1 + mod[:, 7]) + mod[:, 6]
    x = x + 0.5 * mod[:, 8] * swiglu(h, W['ffn_w1'][l, 1], W['ffn_w3'][l, 1], W['ffn_w2'][l, 1])
    return x, state


def setup_inputs(seed: int = 0) -> dict:
    key = jax.random.key(seed)
    ks = iter(jax.random.split(key, 40))
    f32 = jnp.float32
    D = D_MODEL
    d_in = sum(SPLIT_SIZES)

    def nrm(shape, s=1.0):
        return jax.random.normal(next(ks), shape, f32) * s

    dt = jnp.exp(jax.random.uniform(next(ks), (DEPTH, 2, H_C), f32, math.log(1e-3), math.log(1e-1)))
    a_log = jnp.log(jax.random.uniform(next(ks), (DEPTH, 2, H_C), f32, 1.0, 16.0))
    return {
        'x_prompt': nrm((BATCH, SEQ, D)),
        'x_sample': nrm((DEC_BATCH, DEC_SEQ, D)),
        'c': nrm((DEC_BATCH, D)),
        'cache_a_k': nrm((DEC_BATCH, DEPTH, PAST_LEN, KV_A, HEAD_DIM)),
        'cache_a_v': nrm((DEC_BATCH, DEPTH, PAST_LEN, KV_A, HEAD_DIM)),
        'cache_b_k': nrm((DEC_BATCH, DEPTH, PAST_LEN, H_B, HEAD_DIM)),
        'cache_b_v': nrm((DEC_BATCH, DEPTH, PAST_LEN, H_B, HEAD_DIM)),
        'state_c': nrm((DEC_BATCH, DEPTH, 2, H_C, DK, DV), 0.1),
        'c_ctx': nrm((D,)),
        'w_mod': nrm((DEPTH, D, 9 * D), 0.5 * D ** -0.5),
        'b_mod': nrm((DEPTH, 9 * D), 0.01),
        'norm_w': 1.0 + nrm((DEPTH, 3, D), 0.02),
        'ffn_w1': nrm((DEPTH, 2, D, D_FF), D ** -0.5),
        'ffn_w3': nrm((DEPTH, 2, D, D_FF), D ** -0.5),
        'ffn_w2': nrm((DEPTH, 2, D_FF, D), D_FF ** -0.5),
        'w_in': nrm((DEPTH, D, d_in), D ** -0.5),
        'qn_a': 1.0 + nrm((DEPTH, HEAD_DIM), 0.02),
        'kn_a': 1.0 + nrm((DEPTH, HEAD_DIM), 0.02),
        'rpb_b': nrm((DEPTH, H_B, 2 * NA_ROWS - 1, 2 * NA_COLS - 1), 0.1),
        'conv_c': nrm((DEPTH, SHORT_CONV, 3 * MIX_W), SHORT_CONV ** -0.5),
        'a_log_c': a_log,
        'dt_bias_c': dt + jnp.log(-jnp.expm1(-dt)),
        'onorm_c': 1.0 + nrm((DEPTH, DV), 0.02),
        'conv_d': nrm((DEPTH, CONV_D, W_D), CONV_D ** -0.5),
        'conv_d_b': nrm((DEPTH, W_D), 0.01),
        'ln_d_w': 1.0 + nrm((DEPTH, W_D), 0.02),
        'ln_d_b': nrm((DEPTH, W_D), 0.01),
        'w_up': nrm((DEPTH, N_BRANCH, MIX_W, D), MIX_W ** -0.5),
        'w_out': nrm((DEPTH, D, D), D ** -0.5),
        'final_norm': 1.0 + nrm((D,), 0.02),
    }


def reference(x_prompt, x_sample, c, cache_a_k, cache_a_v, cache_b_k, cache_b_v, state_c, c_ctx,
              w_mod, b_mod, norm_w, ffn_w1, ffn_w3, ffn_w2, w_in, qn_a, kn_a, rpb_b, conv_c,
              a_log_c, dt_bias_c, onorm_c, conv_d, conv_d_b, ln_d_w, ln_d_b, w_up, w_out, final_norm):
    W = {'w_mod': w_mod, 'b_mod': b_mod, 'norm_w': norm_w, 'ffn_w1': ffn_w1, 'ffn_w3': ffn_w3,
         'ffn_w2': ffn_w2, 'w_in': w_in, 'qn_a': qn_a, 'kn_a': kn_a, 'rpb_b': rpb_b, 'conv_c': conv_c,
         'a_log_c': a_log_c, 'dt_bias_c': dt_bias_c, 'onorm_c': onorm_c, 'conv_d': conv_d,
         'conv_d_b': conv_d_b, 'ln_d_w': ln_d_w, 'ln_d_b': ln_d_b, 'w_up': w_up, 'w_out': w_out}
    xp = x_prompt
    ctx_cond = c_ctx[None]
    ak_l, av_l, bk_l, bv_l, sc_l = [], [], [], [], []
    for l in range(DEPTH):
        xp, st = trunk_layer(xp, ctx_cond, l, W, None)
        ak_l.append(st[0])
        av_l.append(st[1])
        bk_l.append(st[2])
        bv_l.append(st[3])
        sc_l.append(st[4])
    y_prompt = rms_norm(xp, final_norm)
    xs = x_sample
    for l in range(DEPTH):
        cache = (cache_a_k[:, l], cache_a_v[:, l], cache_b_k[:, l], cache_b_v[:, l], state_c[:, l])
        xs, _ = trunk_layer(xs, c, l, W, cache)
    y_sample = rms_norm(xs, final_norm)
    return (y_prompt, y_sample, jnp.stack(ak_l, axis=1), jnp.stack(av_l, axis=1),
            jnp.stack(bk_l, axis=1), jnp.stack(bv_l, axis=1), jnp.stack(sc_l, axis=1))
```

```python
import functools
import math

import jax
import jax.numpy as jnp
import numpy as np
from jax import lax
from jax.experimental import pallas as pl
from jax.experimental.pallas import tpu as pltpu

F32 = jnp.float32
BF16 = jnp.bfloat16
HI = lax.Precision.HIGHEST

D = 4096
DEPTH = 4
N_CTX, T_CTX = 32, 256
N_LAT, T_LAT = 8, 1024
M_CTX = N_CTX * T_CTX
M = M_CTX + N_LAT * T_LAT
PAST = 256
GRID_W = 64
HD = 128
NH = 8
KV_A = 2
GRP_A = NH // KV_A
MIX_W = 1024
N_BRANCH = 4
NA_ROWS, NA_COLS = 8, 16
CHUNK = 64
CONV_D = 31
D_FF = 11008
ROPE_THETA = 10000.0
EPS = 1e-6
NEG_INF = -1e30
N_COND = 16

OFF_AQ, OFF_BQ, OFF_BK, OFF_BV = 0, 1024, 2048, 3072
OFF_CQ, OFF_CK, OFF_CV, OFF_CZ = 4096, 5120, 6144, 7168
OFF_DA, OFF_DG = 8192, 9216
OFF_AK, OFF_AV = 10240, 10496
OFF_CAB = 10752
P1_W = OFF_CAB + NH * HD
_SPLITS = (1024, 256, 256, 1024, 1024, 1024, 1024, 1024, 1024, 1024, 16, 16, 2048, 4 * D)
_ORIG = dict(zip(('aq', 'ak', 'av', 'bq', 'bk', 'bv', 'cq', 'ck', 'cv', 'cz', 'ca', 'cb', 'dglu', 'gates'),
                 np.concatenate([[0], np.cumsum(_SPLITS)[:-1]]).tolist()))

VMEM_LIMIT = 50 * 1024 * 1024


def _cparams(sem):
    return pltpu.CompilerParams(dimension_semantics=sem, vmem_limit_bytes=VMEM_LIMIT)


def _cond_idx(i, tm):
    n_ctx = M_CTX // tm
    per = T_LAT // tm
    return jnp.where(i < n_ctx, 0, 1 + (i - n_ctx) // per)


def _silu(x):
    return x * jax.nn.sigmoid(x)


def _mod_kernel(c_ref, w_ref, b_ref, o_ref):
    a = _silu(c_ref[...]).astype(BF16)
    o_ref[...] = jnp.dot(a, w_ref[...].astype(BF16), preferred_element_type=F32) + b_ref[...]


def _modulation(cond, w_mod, b_mod):
    tn = 512
    n = 9 * D
    return pl.pallas_call(
        _mod_kernel,
        out_shape=jax.ShapeDtypeStruct((DEPTH, N_COND, n), F32),
        grid=(DEPTH, n // tn),
        in_specs=[pl.BlockSpec((N_COND, D), lambda l, j: (0, 0)),
                  pl.BlockSpec((None, D, tn), lambda l, j: (l, 0, j)),
                  pl.BlockSpec((None, 1, tn), lambda l, j: (l, 0, j))],
        out_specs=pl.BlockSpec((None, N_COND, tn), lambda l, j: (l, 0, j)),
        compiler_params=_cparams(("arbitrary", "arbitrary")),
        name="modulation",
    )(cond, w_mod, b_mod.reshape(DEPTH, 1, n))


def _norm_mod_kernel(x_ref, nw_ref, mod_ref, o_ref, *, k0):
    x = x_ref[...]
    y = x * lax.rsqrt(jnp.mean(x * x, axis=-1, keepdims=True) + EPS) * nw_ref[...]
    o_ref[...] = (y * (1.0 + mod_ref[k0 + 1:k0 + 2, :]) + mod_ref[k0:k0 + 1, :]).astype(o_ref.dtype)


def _norm_mod(x, norm_w, mod, l, s):
    tm = 256
    return pl.pallas_call(
        functools.partial(_norm_mod_kernel, k0=3 * s),
        out_shape=jax.ShapeDtypeStruct((M, D), BF16),
        grid=(M // tm,),
        in_specs=[pl.BlockSpec((tm, D), lambda i: (i, 0)),
                  pl.BlockSpec((None, 1, D), lambda i: (l * 3 + s, 0, 0)),
                  pl.BlockSpec((None, None, 9, D), lambda i: (l, _cond_idx(i, tm), 0, 0))],
        out_specs=pl.BlockSpec((tm, D), lambda i: (i, 0)),
        compiler_params=_cparams(("arbitrary",)),
        name="norm_mod",
    )(x, norm_w.reshape(DEPTH * 3, 1, D), mod)


def _final_norm_kernel(x_ref, w_ref, o_ref):
    x = x_ref[...]
    o_ref[...] = x * lax.rsqrt(jnp.mean(x * x, axis=-1, keepdims=True) + EPS) * w_ref[...]


def _final_norm(x, w):
    tm = 256
    return pl.pallas_call(
        _final_norm_kernel,
        out_shape=jax.ShapeDtypeStruct((M, D), F32),
        grid=(M // tm,),
        in_specs=[pl.BlockSpec((tm, D), lambda i: (i, 0)), pl.BlockSpec((1, D), lambda i: (0, 0))],
        out_specs=pl.BlockSpec((tm, D), lambda i: (i, 0)),
        compiler_params=_cparams(("arbitrary",)),
        name="final_norm",
    )(x, w.reshape(1, D))


def _ffn_up_kernel(h_ref, w1_ref, w3_ref, o_ref):
    h = h_ref[...]
    a = jnp.dot(h, w1_ref[...], preferred_element_type=F32)
    b = jnp.dot(h, w3_ref[...], preferred_element_type=F32)
    o_ref[...] = (_silu(a) * b).astype(o_ref.dtype)


def _ffn_up(h, w1, w3, l, s):
    tm, tn = 1024, 256
    return pl.pallas_call(
        _ffn_up_kernel,
        out_shape=jax.ShapeDtypeStruct((M, D_FF), BF16),
        grid=(M // tm, D_FF // tn),
        in_specs=[pl.BlockSpec((tm, D), lambda i, j: (i, 0)),
                  pl.BlockSpec((None, None, D, tn), lambda i, j: (l, s, 0, j)),
                  pl.BlockSpec((None, None, D, tn), lambda i, j: (l, s, 0, j))],
        out_specs=pl.BlockSpec((tm, tn), lambda i, j: (i, j)),
        compiler_params=_cparams(("arbitrary", "arbitrary")),
        name="ffn_up",
    )(h, w1, w3)


def _res_mm_kernel(a_ref, w_ref, x_ref, mod_ref, o_ref, *, gk, scale):
    y = jnp.dot(a_ref[...], w_ref[...], preferred_element_type=F32)
    o_ref[...] = x_ref[...] + (scale * mod_ref[gk:gk + 1, :]) * y


def _res_mm(a, w, w_idx, x, mod, l, gk, scale, tm, tn):
    k = a.shape[1]
    nlead = len(w_idx)
    return pl.pallas_call(
        functools.partial(_res_mm_kernel, gk=gk, scale=scale),
        out_shape=jax.ShapeDtypeStruct((M, D), F32),
        grid=(M // tm, D // tn),
        in_specs=[pl.BlockSpec((tm, k), lambda i, j: (i, 0)),
                  pl.BlockSpec((None,) * nlead + (k, tn), lambda i, j: tuple(w_idx) + (0, j)),
                  pl.BlockSpec((tm, tn), lambda i, j: (i, j)),
                  pl.BlockSpec((None, None, 9, tn), lambda i, j: (l, _cond_idx(i, tm), 0, j))],
        out_specs=pl.BlockSpec((tm, tn), lambda i, j: (i, j)),
        compiler_params=_cparams(("arbitrary", "arbitrary")),
        name="res_mm",
    )(a, w, x, mod)


def _proj_kernel(h_ref, w_ref, o_ref, *, sigmoid):
    y = jnp.dot(h_ref[...], w_ref[...], preferred_element_type=F32)
    if sigmoid:
        y = jax.nn.sigmoid(y)
    o_ref[...] = y.astype(o_ref.dtype)


def _proj(h, w, l, out_dtype, sigmoid, tn):
    tm = 1024
    n = w.shape[-1]
    return pl.pallas_call(
        functools.partial(_proj_kernel, sigmoid=sigmoid),
        out_shape=jax.ShapeDtypeStruct((M, n), out_dtype),
        grid=(M // tm, n // tn),
        in_specs=[pl.BlockSpec((tm, D), lambda i, j: (i, 0)),
                  pl.BlockSpec((None, D, tn), lambda i, j: (l, 0, j))],
        out_specs=pl.BlockSpec((tm, tn), lambda i, j: (i, j)),
        compiler_params=_cparams(("arbitrary", "arbitrary")),
        name="proj_gates" if sigmoid else "proj_in",
    )(h, w)


def _merge_kernel(ya_ref, yb_ref, yc_ref, yd_ref, w_ref, g0_ref, g1_ref, g2_ref, g3_ref, o_ref):
    acc = None
    for n, (y_ref, g_ref) in enumerate(zip((ya_ref, yb_ref, yc_ref, yd_ref), (g0_ref, g1_ref, g2_ref, g3_ref))):
        t = g_ref[...].astype(F32) * jnp.dot(y_ref[...], w_ref[n], preferred_element_type=F32)
        acc = t if acc is None else acc + t
    o_ref[...] = acc.astype(o_ref.dtype)


def _merge(ys, w_up, gates, l):
    tm, tn = 1024, 512
    nj = D // tn
    y_spec = pl.BlockSpec((tm, MIX_W), lambda i, j: (i, 0))
    g_specs = [pl.BlockSpec((tm, tn), functools.partial(lambda i, j, n: (i, n * nj + j), n=n)) for n in range(N_BRANCH)]
    return pl.pallas_call(
        _merge_kernel,
        out_shape=jax.ShapeDtypeStruct((M, D), BF16),
        grid=(M // tm, nj),
        in_specs=[y_spec] * 4 + [pl.BlockSpec((None, N_BRANCH, MIX_W, tn), lambda i, j: (l, 0, 0, j))] + g_specs,
        out_specs=pl.BlockSpec((tm, tn), lambda i, j: (i, j)),
        compiler_params=_cparams(("arbitrary", "arbitrary")),
        name="merge",
    )(*ys, w_up, gates, gates, gates, gates)


def _head_rms(x, w):
    return x * lax.rsqrt(jnp.mean(x * x, axis=-1, keepdims=True) + EPS) * w


def _rope(x, cos, sin_signed):
    lane = lax.broadcasted_iota(jnp.int32, x.shape, 1)
    partner = jnp.where((lane % 64) < 32, pltpu.roll(x, 96, 1), pltpu.roll(x, 32, 1))
    return x * cos + partner * sin_signed


def _attn_kernel(*refs, t, tq, hps, has_ctx, has_bias, norm, rope, emit_k):
    it = iter(refs)
    q_ref, k_ref, v_ref = next(it), next(it), next(it)
    kc_ref = vc_ref = bias_ref = qn_ref = kn_ref = cos_ref = sin_ref = ko_ref = None
    if has_ctx:
        kc_ref, vc_ref = next(it), next(it)
    if has_bias:
        bias_ref = next(it)
    if norm:
        qn_ref, kn_ref = next(it), next(it)
    if rope:
        cos_ref, sin_ref = next(it), next(it)
    o_ref = next(it)
    if emit_k:
        ko_ref = next(it)
    scale = HD ** -0.5
    nt = (((1,), (1,)), ((), ()))

    k = k_ref[...]
    if norm:
        k = _head_rms(k, kn_ref[...])
    if emit_k:
        ko_ref[...] = k
    if rope:
        k = _rope(k, cos_ref[...], sin_ref[...])
    kb = k.astype(BF16)
    vb = v_ref[...].astype(BF16)
    if has_ctx:
        kcb = kc_ref[...].astype(BF16)
        vcb = vc_ref[...].astype(BF16)
    for qi in range(t // tq):
        rows = pl.ds(qi * tq, tq)
        for j in range(hps):
            cols = pl.ds(j * HD, HD)
            q = q_ref[rows, cols]
            if norm:
                q = _head_rms(q, qn_ref[...])
            if rope:
                q = _rope(q, cos_ref[rows, :], sin_ref[rows, :])
            qb = q.astype(BF16)
            s = lax.dot_general(qb, kb, nt, preferred_element_type=F32) * scale
            if has_bias:
                s = s + bias_ref[rows, :]
            m = jnp.max(s, axis=-1, keepdims=True)
            if has_ctx:
                s2 = lax.dot_general(qb, kcb, nt, preferred_element_type=F32) * scale
                m = jnp.maximum(m, jnp.max(s2, axis=-1, keepdims=True))
            p = jnp.exp(s - m)
            den = jnp.sum(p, axis=-1, keepdims=True)
            o = jnp.dot(p.astype(BF16), vb, preferred_element_type=F32)
            if has_ctx:
                p2 = jnp.exp(s2 - m)
                den = den + jnp.sum(p2, axis=-1, keepdims=True)
                o = o + jnp.dot(p2.astype(BF16), vcb, preferred_element_type=F32)
            o_ref[rows, cols] = (o / den).astype(o_ref.dtype)


def _attention(p1, prev, *, latent, l, q_off, k_off, v_off, kv_group, ctx_k=None, ctx_v=None, bias=None,
               qn=None, kn=None, rope_tabs=None, emit_k=False):
    t = T_LAT if latent else T_CTX
    nb = N_LAT if latent else N_CTX
    r0 = M_CTX // t if latent else 0
    qb, kb, vb = q_off // (kv_group * HD), k_off // HD, v_off // HD
    norm = qn is not None
    qw = kv_group * HD
    in_specs = [pl.BlockSpec((t, qw), lambda g, b: (r0 + b, qb + g)),
                pl.BlockSpec((t, HD), lambda g, b: (r0 + b, kb + g)),
                pl.BlockSpec((t, HD), lambda g, b: (r0 + b, vb + g))]
    args = [p1, p1, p1]
    if ctx_k is not None:
        in_specs += [pl.BlockSpec((None, None, PAST, HD), lambda g, b: (b, l, 0, g))] * 2
        args += [ctx_k, ctx_v]
    if bias is not None:
        assert kv_group == 1
        in_specs.append(pl.BlockSpec((None, None, t, t), lambda g, b: (l, g, 0, 0)))
        args.append(bias)
    if norm:
        in_specs += [pl.BlockSpec((None, 1, HD), lambda g, b: (l, 0, 0))] * 2
        args += [qn, kn]
    if rope_tabs is not None:
        in_specs += [pl.BlockSpec((t, HD), lambda g, b: (0, 0))] * 2
        args += list(rope_tabs)
    out_shape = [jax.ShapeDtypeStruct((M, MIX_W), BF16)]
    out_specs = [pl.BlockSpec((t, qw), lambda g, b: (r0 + b, g))]
    if emit_k:
        out_shape.append(jax.ShapeDtypeStruct((nb * t, (NH // kv_group) * HD), F32))
        out_specs.append(pl.BlockSpec((t, HD), lambda g, b: (b, g)))
    aliases = {}
    if prev is not None:
        in_specs.append(pl.BlockSpec(memory_space=pl.ANY))
        args.append(prev)
        aliases = {len(args) - 1: 0}
    kern = functools.partial(_attn_kernel, t=t, tq=min(t, 256), hps=kv_group, has_ctx=ctx_k is not None,
                             has_bias=bias is not None, norm=norm, rope=rope_tabs is not None, emit_k=emit_k)
    if prev is not None:
        kern = _drop_arg(kern, len(args) - 1)
    return pl.pallas_call(
        kern, out_shape=out_shape, grid=(NH // kv_group, nb), in_specs=in_specs, out_specs=out_specs,
        input_output_aliases=aliases, compiler_params=_cparams(("arbitrary", "arbitrary")),
        name="attn_lat" if latent else "attn_ctx",
    )(*args)


def _drop_arg(kern, idx):
    def wrapped(*refs):
        return kern(*(refs[:idx] + refs[idx + 1:]))
    return wrapped


def _bmm(a, b, spec):
    return jnp.einsum(spec, a, b, precision=HI, preferred_element_type=F32)


def _short_conv_silu(x, w, t):
    row = lax.broadcasted_iota(jnp.int32, x.shape, 0)
    prev = jnp.where(row == 0, 0.0, pltpu.roll(x, 1, 0))
    nxt = jnp.where(row == t - 1, 0.0, pltpu.roll(x, t - 1, 0))
    return _silu(prev * w[0:1, :] + x * w[1:2, :] + nxt * w[2:3, :])


def _l2norm(x):
    return x * lax.rsqrt(jnp.sum(x * x, axis=-1, keepdims=True) + EPS)


def _delta_kernel(*refs, t, cu, has_s0, emit_state):
    it = iter(refs)
    q_ref, k_ref, v_ref, z_ref, ab_ref = next(it), next(it), next(it), next(it), next(it)
    wq_ref, wk_ref, wv_ref, prm_ref, on_ref = next(it), next(it), next(it), next(it), next(it)
    s0_ref = next(it) if has_s0 else None
    o_ref = next(it)
    st_ref = next(it) if emit_state else None
    q_s, k_s, v_s, g_s, b_s = next(it), next(it), next(it), next(it), next(it)
    u_s, w_s, qg_s, kd_s, in_s, eg_s, o_s = (next(it), next(it), next(it), next(it), next(it), next(it), next(it))
    n = t // CHUNK
    c = CHUNK

    q_s[...] = _l2norm(_short_conv_silu(q_ref[...], wq_ref[...], t)) * (HD ** -0.5)
    k_s[...] = _l2norm(_short_conv_silu(k_ref[...], wk_ref[...], t))
    v_s[...] = _short_conv_silu(v_ref[...], wv_ref[...], t)
    ab = ab_ref[...]
    xs = ab + prm_ref[1:2, :]
    softplus = jnp.maximum(xs, 0.0) + jnp.log(1.0 + jnp.exp(-jnp.abs(xs)))
    g_s[...] = -jnp.exp(prm_ref[0:1, :]) * softplus
    b_s[...] = jax.nn.sigmoid(ab)

    ii = lax.broadcasted_iota(jnp.int32, (c, c), 0)
    jj = lax.broadcasted_iota(jnp.int32, (c, c), 1)
    eye = (ii == jj).astype(F32)
    ones = jnp.ones((cu, c, c), F32)

    def local(gi, carry):
        rows = pl.ds(pl.multiple_of(gi * (cu * c), cu * c), cu * c)
        q3 = q_s[rows, :].reshape(cu, c, HD)
        k3 = k_s[rows, :].reshape(cu, c, HD)
        v3 = v_s[rows, :].reshape(cu, c, HD)
        g3 = g_s[rows, :].reshape(cu, c, HD)
        be3 = b_s[rows, :].reshape(cu, c, HD)
        qk = _bmm(q3, k3, 'nid,njd->nij')
        for d in range(2):
            incl = (jj <= ii) if d == 0 else (jj >= ii)
            strict = (jj < ii) if d == 0 else (jj > ii)
            tri = jnp.broadcast_to(incl.astype(F32), (cu, c, c))
            gc = _bmm(tri, g3, 'nij,njd->nid')[:, :, d:d + 1]
            gr = _bmm(ones, gc * eye, 'nit,ntj->nij')
            decay = jnp.where(incl, jnp.exp(jnp.where(incl, gc - gr, 0.0)), 0.0)
            beta = be3[:, :, 2 + d:3 + d]
            kbeta = k3 * beta
            a = jnp.where(strict, _bmm(kbeta, k3, 'nid,njd->nij') * decay, 0.0)
            tinv = eye - a
            pw = a
            for _ in range(5):
                pw = _bmm(pw, pw, 'nij,njk->nik')
                tinv = tinv + _bmm(tinv, pw, 'nij,njk->nik')
            u = _bmm(tinv, v3 * beta, 'nij,njd->nid')
            w = _bmm(tinv, kbeta * jnp.exp(gc), 'nij,njd->nid')
            g_last = gc[:, c - 1:c, :] if d == 0 else gc[:, 0:1, :]
            idx = (d, pl.ds(gi * cu, cu))
            u_s[idx] = u
            w_s[idx] = w
            qg_s[idx] = q3 * jnp.exp(gc)
            kd_s[idx] = k3 * jnp.exp(g_last - gc)
            in_s[idx] = qk * decay
            eg_s[idx] = jnp.broadcast_to(jnp.exp(g_last), (cu, 8, HD))
        return carry

    lax.fori_loop(0, n // cu, local, 0)

    tn_dims = (((0,), (0,)), ((), ()))

    def scan(ci, states):
        new = []
        for d in range(2):
            s = states[d]
            cc = ci if d == 0 else n - 1 - ci
            v_new = u_s[d, cc] - jnp.dot(w_s[d, cc], s, precision=HI, preferred_element_type=F32)
            o = (jnp.dot(qg_s[d, cc], s, precision=HI, preferred_element_type=F32)
                 + jnp.dot(in_s[d, cc], v_new, precision=HI, preferred_element_type=F32))
            o_s[d, cc] = o
            s = s * eg_s[d, cc][0:1, :] + lax.dot_general(kd_s[d, cc], v_new, tn_dims, precision=HI,
                                                          preferred_element_type=F32)
            new.append(s)
        return tuple(new)

    if has_s0:
        init = (s0_ref[0], s0_ref[1])
    else:
        init = (jnp.zeros((HD, HD), F32), jnp.zeros((HD, HD), F32))
    s_f, s_b = lax.fori_loop(0, n, scan, init)
    if emit_state:
        st_ref[0] = s_f
        st_ref[1] = s_b
    o = (o_s[0] + o_s[1]).reshape(t, HD)
    y = o * lax.rsqrt(jnp.mean(o * o, axis=-1, keepdims=True) + EPS) * on_ref[...]
    o_ref[...] = (y * _silu(z_ref[...])).astype(o_ref.dtype)


def _deltanet(p1, prev, *, latent, l, conv_c, prm, onorm, state=None):
    t = T_LAT if latent else T_CTX
    nb = N_LAT if latent else N_CTX
    r0 = M_CTX // t if latent else 0
    n = t // CHUNK
    cu = 4

    def col(off):
        blk = off // HD
        return pl.BlockSpec((t, HD), lambda b, h: (r0 + b, blk + h))

    def cw(part):
        return pl.BlockSpec((None, 3, HD), lambda b, h: (l, 0, part * NH + h))

    in_specs = [col(OFF_CQ), col(OFF_CK), col(OFF_CV), col(OFF_CZ), col(OFF_CAB), cw(0), cw(1), cw(2),
                pl.BlockSpec((None, None, 2, HD), lambda b, h: (l, h, 0, 0)),
                pl.BlockSpec((None, 1, HD), lambda b, h: (l, 0, 0))]
    args = [p1] * 5 + [conv_c] * 3 + [prm, onorm]
    if state is not None:
        in_specs.append(pl.BlockSpec((None, None, 2, None, HD, HD), lambda b, h: (b, l, 0, h, 0, 0)))
        args.append(state)
    out_shape = [jax.ShapeDtypeStruct((M, MIX_W), BF16)]
    out_specs = [pl.BlockSpec((t, HD), lambda b, h: (r0 + b, h))]
    emit_state = not latent
    if emit_state:
        out_shape.append(jax.ShapeDtypeStruct((nb, 2, NH, HD, HD), F32))
        out_specs.append(pl.BlockSpec((None, 2, None, HD, HD), lambda b, h: (b, 0, h, 0, 0)))
    kern = functools.partial(_delta_kernel, t=t, cu=cu, has_s0=state is not None, emit_state=emit_state)
    aliases = {}
    if prev is not None:
        in_specs.append(pl.BlockSpec(memory_space=pl.ANY))
        args.append(prev)
        aliases = {len(args) - 1: 0}
        kern = _drop_arg(kern, len(args) - 1)
    tok = lambda: pltpu.VMEM((t, HD), F32)
    per = lambda w: pltpu.VMEM((2, n, CHUNK, w), F32)
    scratch = [tok(), tok(), tok(), tok(), tok(), per(HD), per(HD), per(HD), per(HD), per(CHUNK),
               pltpu.VMEM((2, n, 8, HD), F32), per(HD)]
    return pl.pallas_call(
        kern, out_shape=out_shape, grid=(nb, NH), in_specs=in_specs, out_specs=out_specs,
        scratch_shapes=scratch, input_output_aliases=aliases,
        compiler_params=_cparams(("arbitrary", "arbitrary")),
        name="delta_lat" if latent else "delta_ctx",
    )(*args)


_CONV_PAD = 16
_CONV_ROWS = 32
_CONV_LANES = 256


def _conformer_kernel(a_ref, g_ref, w_ref, b_ref, lw_ref, lb_ref, o_ref, y_s, c_s, *, t):
    zeros = jnp.zeros((_CONV_PAD, MIX_W), F32)
    y_s[pl.ds(0, _CONV_PAD), :] = zeros
    y_s[pl.ds(_CONV_PAD + t, _CONV_PAD), :] = zeros
    y_s[pl.ds(_CONV_PAD, t), :] = a_ref[...] * jax.nn.sigmoid(g_ref[...])
    shift = _CONV_PAD - CONV_D // 2

    def body(i, carry):
        r = pl.multiple_of(i * _CONV_ROWS, _CONV_ROWS)
        for lt in range(MIX_W // _CONV_LANES):
            lanes = pl.ds(lt * _CONV_LANES, _CONV_LANES)
            win = y_s[pl.ds(r, 2 * _CONV_ROWS), lanes]
            acc = jnp.zeros((_CONV_ROWS, _CONV_LANES), F32)
            for res in range(8):
                wb = win if res == 0 else pltpu.roll(win, 2 * _CONV_ROWS - res, 0)
                for k in range(CONV_D):
                    if (shift + k) % 8 == res:
                        a8 = (shift + k) - res
                        acc = acc + wb[a8:a8 + _CONV_ROWS, :] * w_ref[k:k + 1, lanes]
            c_s[:, lanes] = acc
        y = c_s[...] + b_ref[...]
        mu = jnp.mean(y, axis=-1, keepdims=True)
        yc = y - mu
        var = jnp.mean(yc * yc, axis=-1, keepdims=True)
        y = yc * lax.rsqrt(var + EPS) * lw_ref[...] + lb_ref[...]
        o_ref[pl.ds(r, _CONV_ROWS), :] = _silu(y).astype(o_ref.dtype)
        return carry

    lax.fori_loop(0, t // _CONV_ROWS, body, 0)


def _conformer(p1, prev, *, latent, l, conv_d, conv_d_b, ln_w, ln_b):
    t = T_LAT if latent else T_CTX
    nb = N_LAT if latent else N_CTX
    r0 = M_CTX // t if latent else 0
    vec = pl.BlockSpec((None, 1, MIX_W), lambda b: (l, 0, 0))
    in_specs = [pl.BlockSpec((t, MIX_W), lambda b: (r0 + b, OFF_DA // MIX_W)),
                pl.BlockSpec((t, MIX_W), lambda b: (r0 + b, OFF_DG // MIX_W)),
                pl.BlockSpec((None, CONV_D, MIX_W), lambda b: (l, 0, 0)), vec, vec, vec]
    args = [p1, p1, conv_d, conv_d_b, ln_w, ln_b]
    kern = functools.partial(_conformer_kernel, t=t)
    aliases = {}
    if prev is not None:
        in_specs.append(pl.BlockSpec(memory_space=pl.ANY))
        args.append(prev)
        aliases = {len(args) - 1: 0}
        kern = _drop_arg(kern, len(args) - 1)
    return pl.pallas_call(
        kern, out_shape=jax.ShapeDtypeStruct((M, MIX_W), BF16), grid=(nb,),
        in_specs=in_specs, out_specs=pl.BlockSpec((t, MIX_W), lambda b: (r0 + b, 0)),
        scratch_shapes=[pltpu.VMEM((t + 2 * _CONV_PAD, MIX_W), F32), pltpu.VMEM((_CONV_ROWS, MIX_W), F32)],
        input_output_aliases=aliases,
        compiler_params=_cparams(("arbitrary",)),
        name="conformer_lat" if latent else "conformer_ctx",
    )(*args)


def _prep_w_in(w_in):
    o = _ORIG
    sl = lambda name, width: w_in[:, :, o[name]:o[name] + width]
    ca = sl('ca', 16).reshape(DEPTH, D, 2, NH)
    cb = sl('cb', 16).reshape(DEPTH, D, 2, NH)
    cab = jnp.concatenate([jnp.moveaxis(ca, 2, 3), jnp.moveaxis(cb, 2, 3)], axis=-1)
    cab = jnp.pad(cab, ((0, 0), (0, 0), (0, 0), (0, HD - 4))).reshape(DEPTH, D, NH * HD)
    main = jnp.concatenate([sl('aq', 1024), sl('bq', 1024), sl('bk', 1024), sl('bv', 1024), sl('cq', 1024),
                            sl('ck', 1024), sl('cv', 1024), sl('cz', 1024), sl('dglu', 2048), sl('ak', 256),
                            sl('av', 256), cab], axis=-1).astype(BF16)
    gates = sl('gates', 4 * D).astype(BF16)
    return main, gates


def _rope_tables():
    nf = HD // 4
    pos = jnp.arange(T_LAT)
    inv = 1.0 / (ROPE_THETA ** (jnp.arange(nf, dtype=F32) / nf))
    ar = (pos // GRID_W).astype(F32)[:, None] * inv
    ac = (pos % GRID_W).astype(F32)[:, None] * inv
    cos = jnp.concatenate([jnp.cos(ar), jnp.cos(ar), jnp.cos(ac), jnp.cos(ac)], axis=-1)
    sin = jnp.concatenate([-jnp.sin(ar), jnp.sin(ar), -jnp.sin(ac), jnp.sin(ac)], axis=-1)
    return cos, sin


def _neighbourhood_bias(rpb):
    rows = T_LAT // GRID_W
    wr = min(NA_ROWS, rows)
    pos = np.arange(T_LAT)
    qr, qc = pos // GRID_W, pos % GRID_W
    rs = np.clip(qr - wr // 2, 0, rows - wr)
    cs = np.clip(qc - NA_COLS // 2, 0, GRID_W - NA_COLS)
    kr, kc = qr[None, :], qc[None, :]
    ok = ((kr >= rs[:, None]) & (kr < rs[:, None] + wr) & (kc >= cs[:, None]) & (kc < cs[:, None] + NA_COLS))
    dr = np.clip(kr - qr[:, None] + NA_ROWS - 1, 0, 2 * NA_ROWS - 2)
    dc = np.clip(kc - qc[:, None] + NA_COLS - 1, 0, 2 * NA_COLS - 2)
    bias = rpb[:, :, dr, dc]
    return jnp.where(ok[None, None], bias, NEG_INF)


def kernel(x_prompt, x_sample, c, cache_a_k, cache_a_v, cache_b_k, cache_b_v, state_c, c_ctx, w_mod, b_mod, norm_w, ffn_w1, ffn_w3, ffn_w2, w_in, qn_a, kn_a, rpb_b, conv_c, a_log_c, dt_bias_c, onorm_c, conv_d, conv_d_b, ln_d_w, ln_d_b, w_up, w_out, final_norm):
    x = jnp.concatenate([x_prompt.reshape(M_CTX, D), x_sample.reshape(M - M_CTX, D)], axis=0)
    cond = jnp.concatenate([c_ctx[None], c, jnp.zeros((N_COND - 1 - N_LAT, D), F32)], axis=0)
    mod = _modulation(cond, w_mod, b_mod).reshape(DEPTH, N_COND, 9, D)

    w1, w3, w2 = ffn_w1.astype(BF16), ffn_w3.astype(BF16), ffn_w2.astype(BF16)
    w_main, w_gates = _prep_w_in(w_in)
    w_up_b, w_out_b = w_up.astype(BF16), w_out.astype(BF16)
    rope_tabs = _rope_tables()
    bias_b = _neighbourhood_bias(rpb_b)
    ca_k = cache_a_k.reshape(N_LAT, DEPTH, PAST, KV_A * HD)
    ca_v = cache_a_v.reshape(N_LAT, DEPTH, PAST, KV_A * HD)
    cb_k = cache_b_k.reshape(N_LAT, DEPTH, PAST, NH * HD)
    cb_v = cache_b_v.reshape(N_LAT, DEPTH, PAST, NH * HD)
    qn = qn_a.reshape(DEPTH, 1, HD)
    kn = kn_a.reshape(DEPTH, 1, HD)
    onorm = onorm_c.reshape(DEPTH, 1, HD)
    prm = jnp.stack([jnp.moveaxis(a_log_c, 1, 2), jnp.moveaxis(dt_bias_c, 1, 2)], axis=2)
    prm = jnp.pad(prm, ((0, 0), (0, 0), (0, 0), (0, HD - 2)))
    cd_b = conv_d_b.reshape(DEPTH, 1, MIX_W)
    ln_w = ln_d_w.reshape(DEPTH, 1, MIX_W)
    ln_b = ln_d_b.reshape(DEPTH, 1, MIX_W)

    ak_l, av_l, bk_l, bv_l, st_l = [], [], [], [], []
    for l in range(DEPTH):
        h = _norm_mod(x, norm_w, mod, l, 0)
        act = _ffn_up(h, w1, w3, l, 0)
        x = _res_mm(act, w2, (l, 0), x, mod, l, 2, 0.5, 512, 256)

        h = _norm_mod(x, norm_w, mod, l, 1)
        p1 = _proj(h, w_main, l, F32, False, 512)
        gates = _proj(h, w_gates, l, BF16, True, 512)
        ya, ak_n = _attention(p1, None, latent=False, l=l, q_off=OFF_AQ, k_off=OFF_AK, v_off=OFF_AV,
                              kv_group=GRP_A, qn=qn, kn=kn, emit_k=True)
        ya, = _attention(p1, ya, latent=True, l=l, q_off=OFF_AQ, k_off=OFF_AK, v_off=OFF_AV, kv_group=GRP_A,
                         ctx_k=ca_k, ctx_v=ca_v, qn=qn, kn=kn, rope_tabs=rope_tabs)
        yb, = _attention(p1, None, latent=False, l=l, q_off=OFF_BQ, k_off=OFF_BK, v_off=OFF_BV, kv_group=1)
        yb, = _attention(p1, yb, latent=True, l=l, q_off=OFF_BQ, k_off=OFF_BK, v_off=OFF_BV, kv_group=1,
                         ctx_k=cb_k, ctx_v=cb_v, bias=bias_b)
        yc, st = _deltanet(p1, None, latent=False, l=l, conv_c=conv_c, prm=prm, onorm=onorm)
        yc, = _deltanet(p1, yc, latent=True, l=l, conv_c=conv_c, prm=prm, onorm=onorm, state=state_c)
        yd = _conformer(p1, None, latent=False, l=l, conv_d=conv_d, conv_d_b=cd_b, ln_w=ln_w, ln_b=ln_b)
        yd = _conformer(p1, yd, latent=True, l=l, conv_d=conv_d, conv_d_b=cd_b, ln_w=ln_w, ln_b=ln_b)
        mix = _merge((ya, yb, yc, yd), w_up_b, gates, l)
        x = _res_mm(mix, w_out_b, (l,), x, mod, l, 5, 1.0, 1024, 512)

        h = _norm_mod(x, norm_w, mod, l, 2)
        act = _ffn_up(h, w1, w3, l, 1)
        x = _res_mm(act, w2, (l, 1), x, mod, l, 8, 0.5, 512, 256)

        ak_l.append(ak_n.reshape(N_CTX, T_CTX, KV_A, HD))
        av_l.append(p1[:M_CTX, OFF_AV:OFF_AV + KV_A * HD].reshape(N_CTX, T_CTX, KV_A, HD))
        bk_l.append(p1[:M_CTX, OFF_BK:OFF_BK + MIX_W].reshape(N_CTX, T_CTX, NH, HD))
        bv_l.append(p1[:M_CTX, OFF_BV:OFF_BV + MIX_W].reshape(N_CTX, T_CTX, NH, HD))
        st_l.append(st)

    y = _final_norm(x, final_norm)
    return (y[:M_CTX].reshape(N_CTX, T_CTX, D), y[M_CTX:].reshape(N_LAT, T_LAT, D),
            jnp.stack(ak_l, axis=1), jnp.stack(av_l, axis=1), jnp.stack(bk_l, axis=1), jnp.stack(bv_l, axis=1),
            jnp.stack(st_l, axis=1))
```
